```python
import math
import jax, jax.numpy as jnp
from jax import lax
import numpy as np

D_MODEL = 1024
BATCH = 8
SEQ = 2048
DEPTH = 2
DEC_BATCH = 128
DEC_SEQ = 1
PAST_LEN = 2048
PAGE_SIZE = 128

HEAD_DIM = D_MODEL // 16
HEADS_PER_GROUP = 4
ATTN_GROUPS = ((128, 1), (512, 4), (2048, 16))
N_GROUPS = len(ATTN_GROUPS)
N_HEADS = N_GROUPS * HEADS_PER_GROUP
ATTN_WIDTH = N_HEADS * HEAD_DIM
POOL_WINDOWS = (2, 4, 8, 16)
POOL_WIDTH = D_MODEL // 4
POOL_GROUP_DIM = POOL_WIDTH // len(POOL_WINDOWS)
POOL_STATE = max(POOL_WINDOWS) - 1
D_FF = ((8 * D_MODEL // 3 + 127) // 128) * 128
N_MOD = 9
IN_WIDTH = 3 * ATTN_WIDTH + POOL_WIDTH + 2 * D_MODEL
EPS = 1e-6

kernel_name = "hybrid_dilated_pool_decoder_step"


def alibi_slopes():
    h = np.arange(1, N_HEADS + 1, dtype=np.float32)
    return jnp.asarray(np.power(2.0, -8.0 * h / N_HEADS).astype(np.float32))


def rms_norm(x, g):
    xf = x.astype(jnp.float32)
    y = xf * lax.rsqrt(jnp.mean(xf * xf, axis=-1, keepdims=True) + EPS)
    return (y * g.astype(jnp.float32)).astype(x.dtype)


def adaln_mod(c, w, b):
    m = jax.nn.silu(c) @ w + b
    return m.reshape(c.shape[0], N_MOD, D_MODEL)


def modulate(x, g, shift, scale):
    return rms_norm(x, g) * (1 + scale[:, None, :]) + shift[:, None, :]


def swiglu(h, w1, w2):
    gate, up = jnp.split(h @ w1, 2, axis=-1)
    return (jax.nn.silu(gate) * up) @ w2


def ffn_sublayer(x, mod, i, g, w1, w2):
    h = modulate(x, g, mod[:, 3 * i], mod[:, 3 * i + 1])
    return x + 0.5 * mod[:, 3 * i + 2][:, None, :] * swiglu(h, w1, w2)


def project_in(h, w_in, qn, kn):
    b, s, _ = h.shape
    z = h @ w_in
    cuts = [ATTN_WIDTH, 2 * ATTN_WIDTH, 3 * ATTN_WIDTH,
            3 * ATTN_WIDTH + POOL_WIDTH, 3 * ATTN_WIDTH + POOL_WIDTH + D_MODEL]
    q, k, v, u, ga, gp = jnp.split(z, cuts, axis=-1)
    q = rms_norm(q.reshape(b, s, N_HEADS, HEAD_DIM), qn)
    k = rms_norm(k.reshape(b, s, N_HEADS, HEAD_DIM), kn)
    v = v.reshape(b, s, N_HEADS, HEAD_DIM)
    return q, k, v, u, ga, gp


def band_dilated_attention(q, k, v, window, dil, slopes):
    b, s, h, dh = q.shape
    wsub = window // dil
    L = s // dil
    nb = -(-L // wsub)
    Lp = nb * wsub
    pad_q = ((0, 0), (0, Lp - L), (0, 0), (0, 0), (0, 0))
    pad_k = ((0, 0), (wsub, Lp - L), (0, 0), (0, 0), (0, 0))
    qs = jnp.pad(q.reshape(b, L, dil, h, dh), pad_q).reshape(b, nb, wsub, dil, h, dh)

    def key_blocks(t):
        tb = jnp.pad(t.reshape(b, L, dil, h, dh), pad_k).reshape(b, nb + 1, wsub, dil, h, dh)
        return jnp.concatenate([tb[:, :-1], tb[:, 1:]], axis=2)

    kk = key_blocks(k)
    vv = key_blocks(v)
    sc = jnp.einsum('bnqrhd,bnkrhd->bnrhqk', qs, kk,
                    preferred_element_type=jnp.float32) / math.sqrt(dh)
    qi = jnp.arange(wsub)
    kj = jnp.arange(2 * wsub)
    diff = wsub + qi[:, None] - kj[None, :]
    lk = (jnp.arange(nb)[:, None] - 1) * wsub + kj[None, :]
    valid = ((diff >= 0) & (diff <= wsub))[None] & (lk >= 0)[:, None, :]
    bias = -slopes[:, None, None] * (diff * dil).astype(jnp.float32)[None]
    sc = jnp.where(valid[None, :, None, None], sc + bias[None, None, None], -jnp.inf)
    lse = jax.nn.logsumexp(sc, axis=-1)
    p = jnp.exp(sc - lse[..., None])
    o = jnp.einsum('bnrhqk,bnkrhd->bnqrhd', p.astype(v.dtype), vv)
    o = o.reshape(b, Lp, dil, h, dh)[:, :L].reshape(b, s, h, dh)
    lse = lse.transpose(0, 1, 4, 2, 3).reshape(b, Lp, dil, h)[:, :L].reshape(b, s, h)
    return o, lse


def dilated_attention_step(q, k_all, v_all, window, dil, slopes, n_past):
    t_new = q.shape[1]
    dh = q.shape[-1]
    wsub = window // dil
    offs = dil * jnp.arange(wsub + 1)
    idx = (n_past + jnp.arange(t_new))[:, None] - offs[None, :]
    valid = idx >= 0
    idxc = jnp.maximum(idx, 0)
    kg = k_all[:, idxc]
    vg = v_all[:, idxc]
    sc = jnp.einsum('bthd,btkhd->bthk', q, kg,
                    preferred_element_type=jnp.float32) / math.sqrt(dh)
    bias = -slopes[:, None] * offs.astype(jnp.float32)[None, :]
    sc = jnp.where(valid[None, :, None, :], sc + bias[None, None], -jnp.inf)
    lse = jax.nn.logsumexp(sc, axis=-1)
    p = jnp.exp(sc - lse[..., None])
    o = jnp.einsum('bthk,btkhd->bthd', p.astype(v_all.dtype), vg)
    return o, lse


def pool_mixer(u_ext, pos_ext, n_out, pool_w, pool_scale):
    uf = u_ext.astype(jnp.float32)
    cs = jnp.cumsum(uf, axis=1)
    outs = []
    for gi, w in enumerate(POOL_WINDOWS):
        sl = slice(gi * POOL_GROUP_DIM, (gi + 1) * POOL_GROUP_DIM)
        c = cs[..., sl]
        c_prev = jnp.pad(c, ((0, 0), (w, 0), (0, 0)))[:, :c.shape[1]]
        cnt = jnp.minimum(w, pos_ext + 1).astype(jnp.float32)[None, :, None]
        d = (c - c_prev) / cnt - uf[..., sl]
        outs.append(d[:, -n_out:].astype(u_ext.dtype) @ pool_w[gi])
    return jnp.concatenate(outs, axis=-1) * pool_scale


def merge_out(o_groups, lse_groups, pool_y, ga, gp, w_oa, w_op, w_out):
    b, s = pool_y.shape[:2]
    alpha = jax.nn.softmax(jnp.stack(lse_groups, axis=2), axis=2)
    o = jnp.concatenate([o_g * alpha[:, :, g, :, None].astype(o_g.dtype)
                         for g, o_g in enumerate(o_groups)], axis=2)
    a = o.reshape(b, s, ATTN_WIDTH) @ w_oa
    p = pool_y @ w_op
    return (jax.nn.sigmoid(ga) * a + jax.nn.sigmoid(gp) * p) @ w_out


def setup_inputs(seed: int = 0) -> dict:
    key = jax.random.key(seed)
    ks = jax.random.split(key, 32)
    f32 = jnp.float32

    def nrm(k, shape, s):
        return jax.random.normal(k, shape, f32) * s

    def cshape(w):
        return (DEPTH, DEC_BATCH, min(w, PAST_LEN), HEADS_PER_GROUP, HEAD_DIM)

    w0, w1, w2 = ATTN_GROUPS[0][0], ATTN_GROUPS[1][0], ATTN_GROUPS[2][0]
    return {
        'x_prompt': nrm(ks[0], (BATCH, SEQ, D_MODEL), 1.0),
        'x_sample': nrm(ks[1], (DEC_BATCH, DEC_SEQ, D_MODEL), 1.0),
        'cache_k_g0': nrm(ks[2], cshape(w0), 1.0),
        'cache_v_g0': nrm(ks[3], cshape(w0), 1.0),
        'cache_k_g1': nrm(ks[4], cshape(w1), 1.0),
        'cache_v_g1': nrm(ks[5], cshape(w1), 1.0),
        'cache_k_g2': nrm(ks[6], cshape(w2), 1.0),
        'cache_v_g2': nrm(ks[7], cshape(w2), 1.0),
        'state_pool': nrm(ks[8], (DEPTH, DEC_BATCH, POOL_STATE, POOL_WIDTH), 1.0),
        'c_prompt': nrm(ks[9], (BATCH, D_MODEL), 1.0),
        'c_sample': nrm(ks[10], (DEC_BATCH, D_MODEL), 1.0),
        'w_ada': nrm(ks[11], (DEPTH, D_MODEL, N_MOD * D_MODEL), 0.5 * D_MODEL ** -0.5),
        'b_ada': nrm(ks[12], (DEPTH, N_MOD * D_MODEL), 0.02),
        'norm_g': 1.0 + nrm(ks[13], (DEPTH, 3, D_MODEL), 0.02),
        'ffn_w1': nrm(ks[14], (DEPTH, 2, D_MODEL, 2 * D_FF), D_MODEL ** -0.5),
        'ffn_w2': nrm(ks[15], (DEPTH, 2, D_FF, D_MODEL), D_FF ** -0.5),
        'w_in': nrm(ks[16], (DEPTH, D_MODEL, IN_WIDTH), D_MODEL ** -0.5),
        'q_norm_g': 1.0 + nrm(ks[17], (DEPTH, HEAD_DIM), 0.02),
        'k_norm_g': 1.0 + nrm(ks[18], (DEPTH, HEAD_DIM), 0.02),
        'w_oa': nrm(ks[19], (DEPTH, ATTN_WIDTH, D_MODEL), ATTN_WIDTH ** -0.5),
        'pool_w': nrm(ks[20], (DEPTH, len(POOL_WINDOWS), POOL_GROUP_DIM, POOL_GROUP_DIM), POOL_GROUP_DIM ** -0.5),
        'pool_scale': 1.0 + nrm(ks[21], (DEPTH, POOL_WIDTH), 0.02),
        'w_op': nrm(ks[22], (DEPTH, POOL_WIDTH, D_MODEL), POOL_WIDTH ** -0.5),
        'w_out': nrm(ks[23], (DEPTH, D_MODEL, D_MODEL), D_MODEL ** -0.5),
    }


def reference(x_prompt, x_sample, cache_k_g0, cache_v_g0, cache_k_g1, cache_v_g1,
              cache_k_g2, cache_v_g2, state_pool, c_prompt, c_sample,
              w_ada, b_ada, norm_g, ffn_w1, ffn_w2, w_in, q_norm_g, k_norm_g,
              w_oa, pool_w, pool_scale, w_op, w_out):
    slopes = alibi_slopes()
    cache_k = (cache_k_g0, cache_k_g1, cache_k_g2)
    cache_v = (cache_v_g0, cache_v_g1, cache_v_g2)
    xp, xs = x_prompt, x_sample
    s_len = xp.shape[1]
    t_new = xs.shape[1]
    nkp = [[] for _ in range(N_GROUPS)]
    nvp = [[] for _ in range(N_GROUPS)]
    nks = [[] for _ in range(N_GROUPS)]
    nvs = [[] for _ in range(N_GROUPS)]
    npool_p, npool_s = [], []

    for l in range(DEPTH):
        mod_p = adaln_mod(c_prompt, w_ada[l], b_ada[l])
        mod_s = adaln_mod(c_sample, w_ada[l], b_ada[l])

        xp = ffn_sublayer(xp, mod_p, 0, norm_g[l, 0], ffn_w1[l, 0], ffn_w2[l, 0])
        xs = ffn_sublayer(xs, mod_s, 0, norm_g[l, 0], ffn_w1[l, 0], ffn_w2[l, 0])

        hp = modulate(xp, norm_g[l, 1], mod_p[:, 3], mod_p[:, 4])
        q, k, v, u, ga, gp = project_in(hp, w_in[l], q_norm_g[l], k_norm_g[l])
        o_list, lse_list = [], []
        for g, (win, dil) in enumerate(ATTN_GROUPS):
            hsl = slice(g * HEADS_PER_GROUP, (g + 1) * HEADS_PER_GROUP)
            o, lse = band_dilated_attention(q[:, :, hsl], k[:, :, hsl], v[:, :, hsl],
                                            win, dil, slopes[hsl])
            o_list.append(o)
            lse_list.append(lse)
            keep = min(win, s_len)
            nkp[g].append(k[:, s_len - keep:, hsl])
            nvp[g].append(v[:, s_len - keep:, hsl])
        pos = jnp.arange(s_len)
        yp_pool = pool_mixer(u, pos, s_len, pool_w[l], pool_scale[l])
        npool_p.append(u[:, s_len - POOL_STATE:])
        mix_p = merge_out(o_list, lse_list, yp_pool, ga, gp, w_oa[l], w_op[l], w_out[l])
        xp = xp + mod_p[:, 5][:, None, :] * mix_p

        hs_ = modulate(xs, norm_g[l, 1], mod_s[:, 3], mod_s[:, 4])
        q, k, v, u, ga, gp = project_in(hs_, w_in[l], q_norm_g[l], k_norm_g[l])
        o_list, lse_list = [], []
        for g, (win, dil) in enumerate(ATTN_GROUPS):
            hsl = slice(g * HEADS_PER_GROUP, (g + 1) * HEADS_PER_GROUP)
            n_past = cache_k[g].shape[2]
            k_all = jnp.concatenate([cache_k[g][l], k[:, :, hsl]], axis=1)
            v_all = jnp.concatenate([cache_v[g][l], v[:, :, hsl]], axis=1)
            o, lse = dilated_attention_step(q[:, :, hsl], k_all, v_all, win, dil,
                                            slopes[hsl], n_past)
            o_list.append(o)
            lse_list.append(lse)
            rows = k_all.shape[1]
            keep = min(win, rows)
            nks[g].append(k_all[:, rows - keep:])
            nvs[g].append(v_all[:, rows - keep:])
        u_ext = jnp.concatenate([state_pool[l], u], axis=1)
        pos_ext = PAST_LEN - POOL_STATE + jnp.arange(POOL_STATE + t_new)
        ys_pool = pool_mixer(u_ext, pos_ext, t_new, pool_w[l], pool_scale[l])
        npool_s.append(u_ext[:, u_ext.shape[1] - POOL_STATE:])
        mix_s = merge_out(o_list, lse_list, ys_pool, ga, gp, w_oa[l], w_op[l], w_out[l])
        xs = xs + mod_s[:, 5][:, None, :] * mix_s

        xp = ffn_sublayer(xp, mod_p, 2, norm_g[l, 2], ffn_w1[l, 1], ffn_w2[l, 1])
        xs = ffn_sublayer(xs, mod_s, 2, norm_g[l, 2], ffn_w1[l, 1], ffn_w2[l, 1])

    st = jnp.stack
    return (xp, xs,
            st(nkp[0]), st(nvp[0]), st(nkp[1]), st(nvp[1]), st(nkp[2]), st(nvp[2]), st(npool_p),
            st(nks[0]), st(nvs[0]), st(nks[1]), st(nvs[1]), st(nks[2]), st(nvs[2]), st(npool_s))
```

```python
import functools

import numpy as np
import jax
import jax.numpy as jnp
from jax import lax
from jax.experimental import pallas as pl
from jax.experimental.pallas import tpu as pltpu

F32 = jnp.float32
BF16 = jnp.bfloat16

EPS = 1e-6
HEAD_DIM = 64
HEADS_PER_GROUP = 4
GROUP_WIDTH = HEADS_PER_GROUP * HEAD_DIM
ATTN_GROUPS = ((128, 1), (512, 4), (2048, 16))
N_GROUPS = len(ATTN_GROUPS)
N_HEADS = N_GROUPS * HEADS_PER_GROUP
ATTN_WIDTH = N_HEADS * HEAD_DIM
POOL_WINDOWS = (2, 4, 8, 16)
POOL_GROUP_DIM = 64
POOL_WIDTH = POOL_GROUP_DIM * len(POOL_WINDOWS)
POOL_STATE = max(POOL_WINDOWS) - 1
N_MOD = 9
LANES = 128
QBLK = 128
VMEM_LIMIT_BYTES = 56 * 1024 * 1024
NEG_INF = float("-inf")


def _cparams(*sem):
    return pltpu.CompilerParams(dimension_semantics=sem, vmem_limit_bytes=VMEM_LIMIT_BYTES)


def _sigmoid(x):
    return 1.0 / (1.0 + jnp.exp(-x))


def _dot(a, b):
    return jnp.dot(a, b, preferred_element_type=F32)


def _dot_nt(a, b):
    return lax.dot_general(a, b, (((1,), (1,)), ((), ())), preferred_element_type=F32)


def _modulate(x, g, shift, scale):
    ms = jnp.mean(x * x, axis=-1, keepdims=True)
    y = x * lax.rsqrt(ms + EPS) * g
    return y * (1.0 + scale) + shift


def _resident(shape):
    nd = len(shape)
    return pl.BlockSpec(shape, lambda *_: (0,) * nd, pipeline_mode=pl.Buffered(1))


def _adaln_kernel(c_ref, w_ref, b_ref, o_ref):
    c = c_ref[...]
    a = (c * _sigmoid(c)).astype(BF16)
    o_ref[...] = _dot(a, w_ref[...].astype(BF16)) + b_ref[...]


def _adaln(c_all, w_ada, b_ada):
    depth, d, n = w_ada.shape
    r = c_all.shape[0]
    tn = 1536
    return pl.pallas_call(
        _adaln_kernel,
        grid=(depth, n // tn),
        in_specs=[
            pl.BlockSpec((r, d), lambda l, j: (0, 0)),
            pl.BlockSpec((None, d, tn), lambda l, j: (l, 0, j)),
            pl.BlockSpec((None, 1, tn), lambda l, j: (l, 0, j)),
        ],
        out_specs=pl.BlockSpec((None, r, tn), lambda l, j: (l, 0, j)),
        out_shape=jax.ShapeDtypeStruct((depth, r, n), F32),
        compiler_params=_cparams("parallel", "parallel"),
        name="adaln",
    )(c_all, w_ada, b_ada.reshape(depth, 1, n))


def _mod_specs(per_row, tm, rows_per_batch, d, first):
    def spec(k):
        if per_row:
            return pl.BlockSpec((tm, d), lambda i: (i, k))
        return pl.BlockSpec((None, 1, d), lambda i: ((i * tm) // rows_per_batch, 0, k))
    return [spec(first + j) for j in range(3)]


def _ffn_kernel(x_ref, sh_ref, sc_ref, gt_ref, g_ref, w1_ref, w2_ref, o_ref, h_ref, acc_ref,
                *, d_ff, tf):
    x = x_ref[...]
    h_ref[...] = _modulate(x, g_ref[...], sh_ref[...], sc_ref[...]).astype(BF16)
    acc_ref[...] = jnp.zeros_like(acc_ref)

    def body(j, carry):
        off = pl.multiple_of(j * tf, tf)
        hb = h_ref[...]
        gate = _dot(hb, w1_ref[:, pl.ds(off, tf)])
        up = _dot(hb, w1_ref[:, pl.ds(d_ff + off, tf)])
        a = (gate * _sigmoid(gate) * up).astype(BF16)
        acc_ref[...] += _dot(a, w2_ref[pl.ds(off, tf), :])
        return carry

    lax.fori_loop(0, d_ff // tf, body, 0)
    o_ref[...] = x + 0.5 * gt_ref[...] * acc_ref[...]


def _ffn(x, mod, first, g, w1, w2, *, per_row, rows_per_batch, tm):
    rows, d = x.shape
    d_ff = w2.shape[0]
    tf = 256
    row_spec = pl.BlockSpec((tm, d), lambda i: (i, 0))
    return pl.pallas_call(
        functools.partial(_ffn_kernel, d_ff=d_ff, tf=tf),
        grid=(rows // tm,),
        in_specs=[row_spec] + _mod_specs(per_row, tm, rows_per_batch, d, first) + [
            _resident((1, d)), _resident((d, 2 * d_ff)), _resident((d_ff, d))],
        out_specs=row_spec,
        out_shape=jax.ShapeDtypeStruct((rows, d), F32),
        scratch_shapes=[pltpu.VMEM((tm, d), BF16), pltpu.VMEM((tm, d), F32)],
        compiler_params=_cparams("parallel"),
        name="ffn",
    )(x, mod, mod, mod, g, w1, w2)


def _head_norm_store(z, gn, out_ref, scale):
    lo = lax.broadcasted_iota(jnp.int32, (z.shape[0], LANES), 1) < HEAD_DIM
    for c in range(z.shape[1] // LANES):
        blk = z[:, c * LANES:(c + 1) * LANES]
        sq = blk * blk
        s_lo = jnp.sum(jnp.where(lo, sq, 0.0), axis=1, keepdims=True)
        s_hi = jnp.sum(jnp.where(lo, 0.0, sq), axis=1, keepdims=True)
        ms = jnp.where(lo, s_lo, s_hi) * (1.0 / HEAD_DIM)
        y = blk * lax.rsqrt(ms + EPS) * gn
        out_ref[:, c * LANES:(c + 1) * LANES] = y * scale if scale != 1.0 else y


def _inproj_kernel(x_ref, sh_ref, sc_ref, g_ref, w_ref, qn_ref, kn_ref,
                   q_ref, k_ref, v_ref, u_ref, ga_ref, gp_ref, *, d):
    hb = _modulate(x_ref[...], g_ref[...], sh_ref[...], sc_ref[...]).astype(BF16)
    aw = ATTN_WIDTH
    _head_norm_store(_dot(hb, w_ref[:, 0:aw]), qn_ref[...], q_ref, HEAD_DIM ** -0.5)
    _head_norm_store(_dot(hb, w_ref[:, aw:2 * aw]), kn_ref[...], k_ref, 1.0)
    v_ref[...] = _dot(hb, w_ref[:, 2 * aw:3 * aw])
    c0 = 3 * aw
    u_ref[...] = _dot(hb, w_ref[:, c0:c0 + POOL_WIDTH])
    c1 = c0 + POOL_WIDTH
    ga_ref[...] = _dot(hb, w_ref[:, c1:c1 + d])
    gp_ref[...] = _dot(hb, w_ref[:, c1 + d:c1 + 2 * d])


def _inproj(x, mod, g, w_in, qn, kn, *, per_row, rows_per_batch, tm):
    rows, d = x.shape
    n = w_in.shape[1]
    row_spec = pl.BlockSpec((tm, d), lambda i: (i, 0))
    widths = (ATTN_WIDTH, ATTN_WIDTH, ATTN_WIDTH, POOL_WIDTH, d, d)
    return pl.pallas_call(
        functools.partial(_inproj_kernel, d=d),
        grid=(rows // tm,),
        in_specs=[row_spec] + _mod_specs(per_row, tm, rows_per_batch, d, 3)[:2] + [
            _resident((1, d)), _resident((d, n)), _resident((1, LANES)), _resident((1, LANES))],
        out_specs=[pl.BlockSpec((tm, w), lambda i: (i, 0)) for w in widths],
        out_shape=[jax.ShapeDtypeStruct((rows, w), F32) for w in widths],
        compiler_params=_cparams("parallel"),
        name="inproj",
    )(x, mod, mod, g, w_in, qn, kn)


def _attn_kernel(sl_ref, q0, q1, q2, k0, k1, k2, v0, v1, v2, o0, o1, o2, osc, lsc, *, seq):
    q_refs, k_refs, v_refs, o_refs = (q0, q1, q2), (k0, k1, k2), (v0, v1, v2), (o0, o1, o2)
    jp = pl.program_id(1)
    lo = lax.broadcasted_iota(jnp.int32, (QBLK, LANES), 1) < HEAD_DIM
    qi = lax.broadcasted_iota(jnp.int32, (QBLK, QBLK), 0)
    kj = lax.broadcasted_iota(jnp.int32, (QBLK, QBLK), 1)
    dcur = qi - kj
    dprev = dcur + QBLK

    for g, (_, dil) in enumerate(ATTN_GROUPS):
        nb = seq // (dil * QBLK)
        biases = []
        for a in range(2):
            slope = sl_ref[g * HEADS_PER_GROUP + jp * 2 + a]
            bc = jnp.where(dcur >= 0, -slope * (dcur * dil).astype(F32), NEG_INF)
            bp = jnp.where(dprev <= QBLK, -slope * (dprev * dil).astype(F32), NEG_INF)
            biases.append((bc, bp))
        qr, kr, vr = q_refs[g], k_refs[g], v_refs[g]

        def body(it, carry, dil=dil, nb=nb, biases=biases, qr=qr, kr=kr, vr=vr, g=g):
            r = it // nb
            n = it % nb
            cur = pl.ds(r + dil * QBLK * n, QBLK, stride=dil)
            prv = pl.ds(r + dil * QBLK * jnp.maximum(n - 1, 0), QBLK, stride=dil)
            has_prev = n > 0
            q = qr[cur, :]
            kc = kr[cur, :].astype(BF16)
            kp = kr[prv, :].astype(BF16)
            vc = vr[cur, :].astype(BF16)
            vp = vr[prv, :].astype(BF16)
            outs, lses = [], []
            for a in range(2):
                qa = jnp.where(lo, q, 0.0) if a == 0 else jnp.where(lo, 0.0, q)
                qa = qa.astype(BF16)
                bc, bp = biases[a]
                s_c = _dot_nt(qa, kc) + bc
                s_p = jnp.where(has_prev, _dot_nt(qa, kp) + bp, NEG_INF)
                m = jnp.maximum(jnp.max(s_c, axis=1, keepdims=True),
                                jnp.max(s_p, axis=1, keepdims=True))
                p_c = jnp.exp(s_c - m)
                p_p = jnp.exp(s_p - m)
                l = jnp.sum(p_c, axis=1, keepdims=True) + jnp.sum(p_p, axis=1, keepdims=True)
                o = _dot(p_c.astype(BF16), vc) + _dot(p_p.astype(BF16), vp)
                outs.append(o / l)
                lses.append(m + jnp.log(l))
            osc[g, cur, :] = jnp.where(lo, outs[0], outs[1])
            lsc[g, cur, :] = jnp.where(lo, lses[0], lses[1])
            return carry

        lax.fori_loop(0, seq // QBLK, body, 0)

    l0, l1, l2 = lsc[0], lsc[1], lsc[2]
    mx = jnp.maximum(jnp.maximum(l0, l1), l2)
    e0, e1, e2 = jnp.exp(l0 - mx), jnp.exp(l1 - mx), jnp.exp(l2 - mx)
    inv = 1.0 / (e0 + e1 + e2)
    for g, e in enumerate((e0, e1, e2)):
        o_refs[g][...] = osc[g] * (e * inv)


def _prompt_attention(q, k, v, slopes, *, batch, seq):
    q3 = q.reshape(batch, seq, ATTN_WIDTH)
    k3 = k.reshape(batch, seq, ATTN_WIDTH)
    v3 = v.reshape(batch, seq, ATTN_WIDTH)
    pairs = GROUP_WIDTH // LANES

    def in_spec(g):
        return pl.BlockSpec((None, seq, LANES), lambda b, jp, g=g: (b, 0, g * pairs + jp))

    ins = [in_spec(g) for g in range(N_GROUPS)]
    out_spec = pl.BlockSpec((None, seq, LANES), lambda b, jp: (b, 0, jp))
    outs = pl.pallas_call(
        functools.partial(_attn_kernel, seq=seq),
        grid=(batch, pairs),
        in_specs=[pl.BlockSpec(memory_space=pltpu.SMEM)] + ins * 3,
        out_specs=[out_spec] * N_GROUPS,
        out_shape=[jax.ShapeDtypeStruct((batch, seq, GROUP_WIDTH), F32)] * N_GROUPS,
        scratch_shapes=[pltpu.VMEM((N_GROUPS, seq, LANES), F32)] * 2,
        compiler_params=_cparams("parallel", "parallel"),
        name="prompt_attn",
    )(slopes, q3, q3, q3, k3, k3, k3, v3, v3, v3)
    return [o.reshape(batch * seq, GROUP_WIDTH) for o in outs]


def _pool_group_select(vals, lane):
    grp = lane // POOL_GROUP_DIM
    out = vals[-1]
    for i in range(len(vals) - 2, -1, -1):
        out = jnp.where(grp == i, vals[i], out)
    return out


def _pool_kernel(u_ref, w_ref, sc_ref, y_ref):
    u = u_ref[...]
    t = lax.broadcasted_iota(jnp.int32, u.shape, 0)
    lane = lax.broadcasted_iota(jnp.int32, u.shape, 1)

    def back(x, k):
        return jnp.where(t >= k, pltpu.roll(x, k, axis=0), 0.0)

    sums = []
    s = u
    for w in POOL_WINDOWS:
        s = s + back(s, w // 2)
        sums.append(s)
    win = _pool_group_select([jnp.full(u.shape, w, jnp.int32) for w in POOL_WINDOWS], lane)
    cnt = jnp.minimum(win, t + 1).astype(F32)
    dlt = _pool_group_select(sums, lane) / cnt - u
    y_ref[...] = _dot(dlt.astype(BF16), w_ref[...]) * sc_ref[...]


def _pool_prompt(u, w_bd, scale, *, batch, seq):
    u3 = u.reshape(batch, seq, POOL_WIDTH)
    spec = pl.BlockSpec((None, seq, POOL_WIDTH), lambda b: (b, 0, 0))
    y = pl.pallas_call(
        _pool_kernel,
        grid=(batch,),
        in_specs=[spec, _resident((POOL_WIDTH, POOL_WIDTH)), _resident((1, POOL_WIDTH))],
        out_specs=spec,
        out_shape=jax.ShapeDtypeStruct((batch, seq, POOL_WIDTH), F32),
        compiler_params=_cparams("parallel"),
        name="pool_prompt",
    )(u3, w_bd, scale)
    return y.reshape(batch * seq, POOL_WIDTH)


def _pool_sample_kernel(st_ref, u_ref, w_ref, sc_ref, y_ref, ns_ref, *, counts):
    u = u_ref[...]
    lane = lax.broadcasted_iota(jnp.int32, u.shape, 1)
    sums = []
    s = u
    for i in range(1, POOL_STATE + 1):
        s = s + st_ref[POOL_STATE - i]
        if i + 1 in POOL_WINDOWS:
            sums.append(s * (1.0 / counts[POOL_WINDOWS.index(i + 1)]))
    dlt = _pool_group_select(sums, lane) - u
    y_ref[...] = _dot(dlt.astype(BF16), w_ref[...]) * sc_ref[...]
    ns_ref[0:POOL_STATE - 1] = st_ref[1:POOL_STATE]
    ns_ref[POOL_STATE - 1] = u


def _pool_sample(state_t, layer, u, w_bd, scale, *, past_len):
    _, ps, db, pw = state_t.shape
    counts = tuple(float(min(w, past_len + 1)) for w in POOL_WINDOWS)
    return pl.pallas_call(
        functools.partial(_pool_sample_kernel, counts=counts),
        grid=(1,),
        in_specs=[pl.BlockSpec((None, ps, db, pw), lambda i: (layer, 0, 0, 0)),
                  _resident((db, pw)), _resident((pw, pw)), _resident((1, pw))],
        out_specs=[pl.BlockSpec((db, pw), lambda i: (0, 0)),
                   pl.BlockSpec((ps, db, pw), lambda i: (0, 0, 0))],
        out_shape=[jax.ShapeDtypeStruct((db, pw), F32),
                   jax.ShapeDtypeStruct((ps, db, pw), F32)],
        compiler_params=_cparams("arbitrary"),
        name="pool_sample",
    )(state_t, u, w_bd, scale)


def _merge_kernel(x_ref, gt_ref, o0, o1, o2, py_ref, ga_ref, gp_ref, woa_ref, wop_ref, wout_ref,
                  out_ref):
    a = None
    for g, o in enumerate((o0, o1, o2)):
        t = _dot(o[...].astype(BF16), woa_ref[g * GROUP_WIDTH:(g + 1) * GROUP_WIDTH, :])
        a = t if a is None else a + t
    p = _dot(py_ref[...].astype(BF16), wop_ref[...])
    m = _sigmoid(ga_ref[...]) * a + _sigmoid(gp_ref[...]) * p
    out_ref[...] = x_ref[...] + gt_ref[...] * _dot(m.astype(BF16), wout_ref[...])


def _merge(x, mod, o_groups, py, ga, gp, w_oa, w_op, w_out, *, per_row, rows_per_batch, tm):
    rows, d = x.shape

    def rs(w):
        return pl.BlockSpec((tm, w), lambda i: (i, 0))

    return pl.pallas_call(
        _merge_kernel,
        grid=(rows // tm,),
        in_specs=[rs(d), _mod_specs(per_row, tm, rows_per_batch, d, 5)[0]]
        + [rs(GROUP_WIDTH)] * N_GROUPS + [rs(POOL_WIDTH), rs(d), rs(d),
                                          _resident(w_oa.shape), _resident(w_op.shape),
                                          _resident(w_out.shape)],
        out_specs=rs(d),
        out_shape=jax.ShapeDtypeStruct((rows, d), F32),
        compiler_params=_cparams("parallel"),
        name="merge",
    )(x, mod, *o_groups, py, ga, gp, w_oa, w_op, w_out)


def _decode_kernel(*refs, aliased):
    sl_ref, qt_ref, kt_ref, vt_ref = refs[0:4]
    kin, vin = refs[4:7], refs[7:10]
    base = 10 + (2 * N_GROUPS if aliased else 0)
    kout, vout = refs[base:base + 3], refs[base + 3:base + 6]
    ot_ref = refs[base + 6]

    b = pl.program_id(0)
    sel = lax.broadcasted_iota(jnp.int32, qt_ref.shape, 1) == b

    def column(ref):
        return jnp.sum(jnp.where(sel, ref[...], 0.0), axis=1, keepdims=True)

    qcol, kcol, vcol = column(qt_ref), column(kt_ref), column(vt_ref)

    @pl.when(b == 0)
    def _():
        ot_ref[...] = jnp.zeros_like(ot_ref)

    o_heads, lse_heads = [], []
    for g, (_, dil) in enumerate(ATTN_GROUPS):
        kt = kin[g][...]
        vt = vin[g][...]
        n = kt.shape[1]
        pos = lax.broadcasted_iota(jnp.int32, (1, n), 1)
        dist = n - pos
        valid = (dist % dil) == 0
        distf = dist.astype(F32)
        for h in range(HEADS_PER_GROUP):
            r0 = g * GROUP_WIDTH + h * HEAD_DIM
            rows = slice(h * HEAD_DIM, (h + 1) * HEAD_DIM)
            qc = qcol[r0:r0 + HEAD_DIM]
            kc = kcol[r0:r0 + HEAD_DIM]
            vc = vcol[r0:r0 + HEAD_DIM]
            slope = sl_ref[g * HEADS_PER_GROUP + h]
            s = jnp.sum(kt[rows] * qc, axis=0, keepdims=True)
            s = jnp.where(valid, s - slope * distf, NEG_INF)
            s_new = jnp.sum(qc * kc, axis=0, keepdims=True)
            m = jnp.maximum(jnp.max(s, axis=1, keepdims=True), s_new)
            p = jnp.exp(s - m)
            p_new = jnp.exp(s_new - m)
            l = jnp.sum(p, axis=1, keepdims=True) + p_new
            o = jnp.sum(vt[rows] * p, axis=1, keepdims=True) + p_new * vc
            o_heads.append(o / l)
            lse_heads.append(m + jnp.log(l))
        last = lax.broadcasted_iota(jnp.int32, kt.shape, 1) == n - 1
        kc_g = kcol[g * GROUP_WIDTH:(g + 1) * GROUP_WIDTH]
        vc_g = vcol[g * GROUP_WIDTH:(g + 1) * GROUP_WIDTH]
        kout[g][...] = jnp.where(last, kc_g, pltpu.roll(kt, n - 1, axis=1))
        vout[g][...] = jnp.where(last, vc_g, pltpu.roll(vt, n - 1, axis=1))

    cols = [None] * N_HEADS
    for h in range(HEADS_PER_GROUP):
        ls = [lse_heads[g * HEADS_PER_GROUP + h] for g in range(N_GROUPS)]
        mx = jnp.maximum(jnp.maximum(ls[0], ls[1]), ls[2])
        es = [jnp.exp(x - mx) for x in ls]
        inv = 1.0 / (es[0] + es[1] + es[2])
        for g in range(N_GROUPS):
            cols[g * HEADS_PER_GROUP + h] = o_heads[g * HEADS_PER_GROUP + h] * (es[g] * inv)
    ocol = jnp.concatenate(cols, axis=0)
    ot_ref[...] = jnp.where(sel, ocol, ot_ref[...])


def _decode(layer, slopes, qt, kt_new, vt_new, caches_k, caches_v, prev):
    depth, db = caches_k[0].shape[:2]
    aliased = prev is not None
    tbl = _resident(qt.shape)

    def cspec(n):
        return pl.BlockSpec((None, None, GROUP_WIDTH, n), lambda b: (layer, b, 0, 0))

    cache_specs = [cspec(c.shape[3]) for c in caches_k] + [cspec(c.shape[3]) for c in caches_v]
    in_specs = [pl.BlockSpec(memory_space=pltpu.SMEM), tbl, tbl, tbl] + cache_specs
    args = [slopes, qt, kt_new, vt_new, *caches_k, *caches_v]
    aliases = {}
    if aliased:
        in_specs += [pl.BlockSpec(memory_space=pl.ANY)] * (2 * N_GROUPS)
        aliases = {len(args) + i: i for i in range(2 * N_GROUPS)}
        args += list(prev)
    outs = pl.pallas_call(
        functools.partial(_decode_kernel, aliased=aliased),
        grid=(db,),
        in_specs=in_specs,
        out_specs=cache_specs + [pl.BlockSpec(qt.shape, lambda b: (0, 0))],
        out_shape=[jax.ShapeDtypeStruct(c.shape, F32) for c in (*caches_k, *caches_v)]
        + [jax.ShapeDtypeStruct(qt.shape, F32)],
        input_output_aliases=aliases,
        compiler_params=_cparams("arbitrary"),
        name="decode",
    )(*args)
    return outs[:2 * N_GROUPS], outs[2 * N_GROUPS]


def _alibi_slopes():
    h = np.arange(1, N_HEADS + 1, dtype=np.float32)
    return jnp.asarray(np.power(2.0, -8.0 * h / N_HEADS).astype(np.float32))


def _block_diag(pool_w_l):
    n = pool_w_l.shape[0]
    rows = [jnp.concatenate([pool_w_l[i] if i == j else jnp.zeros_like(pool_w_l[i])
                             for j in range(n)], axis=1) for i in range(n)]
    return jnp.concatenate(rows, axis=0)


def _cache_view(c):
    depth, db, n = c.shape[:3]
    return jnp.transpose(c, (0, 1, 3, 4, 2)).reshape(depth, db, GROUP_WIDTH, n)


def _cache_unview(c):
    depth, db, _, n = c.shape
    return jnp.transpose(c.reshape(depth, db, HEADS_PER_GROUP, HEAD_DIM, n), (0, 1, 4, 2, 3))


def kernel(x_prompt, x_sample, cache_k_g0, cache_v_g0, cache_k_g1, cache_v_g1, cache_k_g2, cache_v_g2, state_pool, c_prompt, c_sample, w_ada, b_ada, norm_g, ffn_w1, ffn_w2, w_in, q_norm_g, k_norm_g, w_oa, pool_w, pool_scale, w_op, w_out):
    batch, seq, d = x_prompt.shape
    db = x_sample.shape[0]
    depth = w_ada.shape[0]
    assert x_sample.shape[1] == 1 and seq % (ATTN_GROUPS[-1][1] * QBLK) == 0
    past_len = cache_k_g2.shape[2]
    assert all(c.shape[2] == w for c, (w, _) in
               zip((cache_k_g0, cache_k_g1, cache_k_g2), ATTN_GROUPS))

    slopes = _alibi_slopes()
    xp = x_prompt.reshape(batch * seq, d)
    xs = x_sample.reshape(db, d)
    mod = _adaln(jnp.concatenate([c_prompt, c_sample], axis=0), w_ada, b_ada)
    w1b, w2b, winb = ffn_w1.astype(BF16), ffn_w2.astype(BF16), w_in.astype(BF16)
    woab, wopb, woutb = w_oa.astype(BF16), w_op.astype(BF16), w_out.astype(BF16)
    caches_k = [_cache_view(c) for c in (cache_k_g0, cache_k_g1, cache_k_g2)]
    caches_v = [_cache_view(c) for c in (cache_v_g0, cache_v_g1, cache_v_g2)]
    state_t = jnp.transpose(state_pool, (0, 2, 1, 3))

    p_kw = dict(per_row=False, rows_per_batch=seq, tm=512)
    s_kw = dict(per_row=True, rows_per_batch=1, tm=db)
    nkp = [[] for _ in range(N_GROUPS)]
    nvp = [[] for _ in range(N_GROUPS)]
    npool_p, npool_s = [], []
    dec_bufs = None

    for l in range(depth):
        mod_p = mod[l, :batch].reshape(batch, 1, N_MOD * d)
        mod_s = mod[l, batch:]
        g = [norm_g[l, i].reshape(1, d) for i in range(3)]
        qn = jnp.tile(q_norm_g[l], 2).reshape(1, LANES)
        kn = jnp.tile(k_norm_g[l], 2).reshape(1, LANES)
        w_bd = _block_diag(pool_w[l]).astype(BF16)
        pscale = pool_scale[l].reshape(1, POOL_WIDTH)

        xp = _ffn(xp, mod_p, 0, g[0], w1b[l, 0], w2b[l, 0], **p_kw)
        xs = _ffn(xs, mod_s, 0, g[0], w1b[l, 0], w2b[l, 0], **s_kw)

        q, k, v, u, ga, gp = _inproj(xp, mod_p, g[1], winb[l], qn, kn, **p_kw)
        o_groups = _prompt_attention(q, k, v, slopes, batch=batch, seq=seq)
        py = _pool_prompt(u, w_bd, pscale, batch=batch, seq=seq)
        xp = _merge(xp, mod_p, o_groups, py, ga, gp, woab[l], wopb[l], woutb[l], **p_kw)
        k4 = k.reshape(batch, seq, N_HEADS, HEAD_DIM)
        v4 = v.reshape(batch, seq, N_HEADS, HEAD_DIM)
        for gi, (win, _) in enumerate(ATTN_GROUPS):
            keep = min(win, seq)
            hsl = slice(gi * HEADS_PER_GROUP, (gi + 1) * HEADS_PER_GROUP)
            nkp[gi].append(k4[:, seq - keep:, hsl])
            nvp[gi].append(v4[:, seq - keep:, hsl])
        npool_p.append(u.reshape(batch, seq, POOL_WIDTH)[:, seq - POOL_STATE:])

        q, k, v, u, ga, gp = _inproj(xs, mod_s, g[1], winb[l], qn, kn, **s_kw)
        dec_bufs, o_t = _decode(l, slopes, q.T, k.T, v.T, caches_k, caches_v, dec_bufs)
        py, new_state = _pool_sample(state_t, l, u, w_bd, pscale, past_len=past_len)
        npool_s.append(new_state)
        o_s = o_t.T
        o_groups = [o_s[:, gi * GROUP_WIDTH:(gi + 1) * GROUP_WIDTH] for gi in range(N_GROUPS)]
        xs = _merge(xs, mod_s, o_groups, py, ga, gp, woab[l], wopb[l], woutb[l], **s_kw)

        xp = _ffn(xp, mod_p, 6, g[2], w1b[l, 1], w2b[l, 1], **p_kw)
        xs = _ffn(xs, mod_s, 6, g[2], w1b[l, 1], w2b[l, 1], **s_kw)

    st = jnp.stack
    new_s = [_cache_unview(c) for c in dec_bufs]
    new_pool_s = jnp.transpose(st(npool_s), (0, 2, 1, 3))
    return (xp.reshape(batch, seq, d), xs.reshape(db, 1, d),
            st(nkp[0]), st(nvp[0]), st(nkp[1]), st(nvp[1]), st(nkp[2]), st(nvp[2]), st(npool_p),
            new_s[0], new_s[3], new_s[1], new_s[4], new_s[2], new_s[5], new_pool_s)
```
